```python
import math
import jax, jax.numpy as jnp
from jax import lax
import numpy as np

D_MODEL = 1024
BATCH = 1
SEQ = 16384
DEPTH = 2
DEC_BATCH = 32
DEC_SEQ = 8
PAST_LEN = 16384
PAGE_SIZE = 128

N_EVEN = (DEPTH + 1) // 2
N_ODD = DEPTH // 2
A_WIDTH = D_MODEL // 2
A_GROUPS = 4
A_CHUNK = 128
B_HEADS = 4
B_DK = 128
B_DV = 128
GLA_CHUNK = 64
C_HEADS = 8
C_DH = 64
Q_BLOCK = 128
D_FF = 4 * D_MODEL
ALPHA = (2 * DEPTH) ** 0.25
BETA = (8 * DEPTH) ** -0.25
LN_EPS = 1e-5
RMS_EPS = 1e-5

kernel_name = 'hybrid_gmlp_hgrn2_diffattn_decode_step'


def layer_norm(x, g, b):
    xf = x.astype(jnp.float32)
    mu = jnp.mean(xf, -1, keepdims=True)
    xc = xf - mu
    var = jnp.mean(xc * xc, -1, keepdims=True)
    return (xc * lax.rsqrt(var + LN_EPS) * g.astype(jnp.float32) + b.astype(jnp.float32)).astype(x.dtype)


def rms_norm(x, w):
    xf = x.astype(jnp.float32)
    return xf * lax.rsqrt(jnp.mean(xf * xf, -1, keepdims=True) + RMS_EPS) * w.astype(jnp.float32)


def sq_relu_mlp(x, w1, w2):
    h = jax.nn.relu(x @ w1)
    return (h * h) @ w2


def lambda_init(layer_idx):
    return 0.8 - 0.6 * math.exp(-0.3 * layer_idx)


def chunk_spatial_gate(v, w_s, b_s):
    n, t, c = v.shape
    tp = -(-t // A_CHUNK) * A_CHUNK
    vc = jnp.pad(v, ((0, 0), (0, tp - t), (0, 0))).reshape(n, tp // A_CHUNK, A_CHUNK, A_GROUPS, c // A_GROUPS)
    causal = jnp.tril(jnp.ones((A_CHUNK, A_CHUNK), bool))
    w = jnp.where(causal, w_s, 0.0)
    m = jnp.einsum('gts,ncsgd->nctgd', w, vc) + b_s.T[:, :, None]
    return m.reshape(n, tp, c)[:, :t]


def gla_chunked(q, k, v, log_f, s0):
    n, t, h, _ = q.shape
    dv = v.shape[-1]
    c = min(GLA_CHUNK, t)
    tp = -(-t // c) * c

    def to_chunks(a):
        a = jnp.pad(a, ((0, 0), (0, tp - t), (0, 0), (0, 0)))
        return a.reshape(n, tp // c, c, h, a.shape[-1]).transpose(1, 0, 2, 3, 4)

    causal = jnp.tril(jnp.ones((c, c), bool))[None, :, :, None, None]

    def step(s, inp):
        qc, kc, vc, gc = inp
        b = jnp.cumsum(gc, axis=1)
        o_inter = jnp.einsum('nthk,nhkv->nthv', qc * jnp.exp(b), s)
        decay = jnp.exp(jnp.where(causal, b[:, :, None] - b[:, None, :], -jnp.inf))
        scores = jnp.einsum('nthk,ntshk,nshk->nhts', qc, decay, kc)
        o_intra = jnp.einsum('nhts,nshv->nthv', scores, vc)
        b_end = b[:, -1]
        s_new = jnp.exp(b_end)[..., None] * s + jnp.einsum('nshk,nshv->nhkv', kc * jnp.exp(b_end[:, None] - b), vc)
        return s_new, o_inter + o_intra

    s_fin, o = lax.scan(step, s0, (to_chunks(q), to_chunks(k), to_chunks(v), to_chunks(log_f)))
    o = o.transpose(1, 0, 2, 3, 4).reshape(n, tp, h, dv)[:, :t]
    return o, s_fin


def even_mixer(x, s0, lb, w_in, w_s, b_s, vn_g, vn_b, gn_w, w_out):
    n, t, _ = x.shape
    z = x @ w_in
    wq = B_HEADS * B_DK
    wv = B_HEADS * B_DV
    o0 = 2 * A_WIDTH
    u = z[..., :A_WIDTH]
    v = z[..., A_WIDTH:o0]
    q = z[..., o0:o0 + wq]
    f = z[..., o0 + wq:o0 + 2 * wq]
    i = z[..., o0 + 2 * wq:o0 + 2 * wq + wv]
    g = z[..., o0 + 2 * wq + wv:]
    u = jax.nn.gelu(u, approximate=False)
    v = layer_norm(jax.nn.gelu(v, approximate=False), vn_g, vn_b)
    a_out = u * chunk_spatial_gate(v, w_s, b_s).astype(x.dtype)
    qh = jax.nn.silu(q.astype(jnp.float32)).reshape(n, t, B_HEADS, B_DK)
    fg = lb + (1.0 - lb) * jax.nn.sigmoid(f.astype(jnp.float32))
    fg = fg.reshape(n, t, B_HEADS, B_DK)
    o, s_fin = gla_chunked(qh, 1.0 - fg, i.astype(jnp.float32).reshape(n, t, B_HEADS, B_DV), jnp.log(fg), s0)
    o = rms_norm(o, gn_w) * jax.nn.silu(g.astype(jnp.float32).reshape(n, t, B_HEADS, B_DV))
    b_out = o.reshape(n, t, wv).astype(x.dtype)
    y = jnp.concatenate([a_out, b_out], axis=-1) @ w_out
    return y, v, s_fin


def diff_lambda(lq1, lk1, lq2, lk2, lam_init):
    f32 = lambda a: a.astype(jnp.float32)
    return jnp.exp(jnp.sum(f32(lq1) * f32(lk1))) - jnp.exp(jnp.sum(f32(lq2) * f32(lk2))) + lam_init


def qkv_project(x, w_qkv):
    n, t, _ = x.shape
    q, k, v = jnp.split(x @ w_qkv, 3, axis=-1)
    return (q.reshape(n, t, C_HEADS, 2, C_DH),
            k.reshape(n, t, C_HEADS, 2 * C_DH),
            v.reshape(n, t, C_HEADS, 2 * C_DH))


def diff_attn_core(q, q_pos, k, v, k_pos, lam):
    n, kl = k.shape[:2]
    k = k.reshape(n, kl, C_HEADS, 2, C_DH)
    s = jnp.einsum('nqhjd,nkhjd->nhjqk', q, k, preferred_element_type=jnp.float32) * (C_DH ** -0.5)
    s = jnp.where(k_pos[None, :] <= q_pos[:, None], s, -jnp.inf)
    p = jax.nn.softmax(s, axis=-1)
    w = p[:, :, 0] - lam * p[:, :, 1]
    return jnp.einsum('nhqk,nkhe->nqhe', w, v.astype(jnp.float32))


def prompt_diff_attention(q, k, v, lam):
    n, t = q.shape[:2]
    nb = t // Q_BLOCK
    qb = q.reshape(n, nb, Q_BLOCK, C_HEADS, 2, C_DH).transpose(1, 0, 2, 3, 4, 5)
    qpos = jnp.arange(t).reshape(nb, Q_BLOCK)
    kpos = jnp.arange(t)
    o = lax.map(lambda a: diff_attn_core(a[0], a[1], k, v, kpos, lam), (qb, qpos))
    return o.transpose(1, 0, 2, 3, 4).reshape(n, t, C_HEADS, 2 * C_DH)


def sample_diff_attention(q, k, v, cache_k, cache_v, j, page_table, lam):
    t = q.shape[1]
    past = page_table.shape[1] * cache_k.shape[2]
    qpos = past + jnp.arange(t)
    kpos = jnp.arange(past + t)

    def one(a):
        qn, kn, vn, pt = a
        kp = cache_k[j, pt].reshape(past, C_HEADS, 2 * C_DH).astype(kn.dtype)
        vp = cache_v[j, pt].reshape(past, C_HEADS, 2 * C_DH).astype(vn.dtype)
        k_all = jnp.concatenate([kp, kn], axis=0)[None]
        v_all = jnp.concatenate([vp, vn], axis=0)[None]
        return diff_attn_core(qn[None], qpos, k_all, v_all, kpos, lam)[0]

    return lax.map(one, (q, k, v, page_table))


def diff_out(o, sub_w, w_out, lam_init, dtype):
    n, t = o.shape[:2]
    o = rms_norm(o, sub_w) * (1.0 - lam_init)
    return o.reshape(n, t, C_HEADS * 2 * C_DH).astype(dtype) @ w_out


def setup_inputs(seed: int = 0) -> dict:
    key = jax.random.key(seed)
    ks = jax.random.split(key, 32)
    n_pages = PAST_LEN // PAGE_SIZE
    n_used = DEC_BATCH * n_pages
    n_phys = n_used + n_used // 4
    nrm = lambda k, shape, scale: jax.random.normal(k, shape, jnp.float32) * scale
    wq = B_HEADS * B_DK
    wv = B_HEADS * B_DV
    even_in = 2 * A_WIDTH + 2 * wq + 2 * wv
    even_out = A_WIDTH + wv
    c_w = C_HEADS * 2 * C_DH
    even_cols = jnp.concatenate([jnp.ones((2 * A_WIDTH + 2 * wq,), jnp.float32),
                                 jnp.full((wv,), BETA, jnp.float32), jnp.ones((wv,), jnp.float32)])
    odd_cols = jnp.concatenate([jnp.ones((2 * c_w,), jnp.float32), jnp.full((c_w,), BETA, jnp.float32)])
    return {
        'x_prompt': nrm(ks[0], (BATCH, SEQ, D_MODEL), 1.0),
        'x_sample': nrm(ks[1], (DEC_BATCH, DEC_SEQ, D_MODEL), 1.0),
        'cache_k': nrm(ks[2], (N_ODD, n_phys, PAGE_SIZE, C_HEADS, 2 * C_DH), 1.0),
        'cache_v': nrm(ks[3], (N_ODD, n_phys, PAGE_SIZE, C_HEADS, 2 * C_DH), 1.0),
        'state_hgrn': nrm(ks[4], (N_EVEN, DEC_BATCH, B_HEADS, B_DK, B_DV), 0.5),
        'page_table': jax.random.permutation(ks[5], n_phys)[:n_used].reshape(DEC_BATCH, n_pages).astype(jnp.int32),
        'ln_g': 1.0 + nrm(ks[6], (DEPTH, 2, D_MODEL), 0.02),
        'ln_b': nrm(ks[7], (DEPTH, 2, D_MODEL), 0.02),
        'w_in_even': nrm(ks[8], (N_EVEN, D_MODEL, even_in), D_MODEL ** -0.5) * even_cols,
        'w_s': nrm(ks[9], (N_EVEN, A_GROUPS, A_CHUNK, A_CHUNK), A_CHUNK ** -0.5),
        'b_s': 1.0 + nrm(ks[10], (N_EVEN, A_GROUPS, A_CHUNK), 0.02),
        'vnorm_g': 1.0 + nrm(ks[11], (N_EVEN, A_WIDTH), 0.02),
        'vnorm_b': nrm(ks[12], (N_EVEN, A_WIDTH), 0.02),
        'lb_logits': nrm(ks[13], (DEPTH + 1, wq), 0.1),
        'gnorm_w': 1.0 + nrm(ks[14], (N_EVEN, B_DV), 0.02),
        'w_out_even': nrm(ks[15], (N_EVEN, even_out, D_MODEL), even_out ** -0.5 * BETA),
        'w_qkv': nrm(ks[16], (N_ODD, D_MODEL, 3 * c_w), D_MODEL ** -0.5) * odd_cols,
        'lam_q1': nrm(ks[17], (N_ODD, C_DH), 0.1),
        'lam_k1': nrm(ks[18], (N_ODD, C_DH), 0.1),
        'lam_q2': nrm(ks[19], (N_ODD, C_DH), 0.1),
        'lam_k2': nrm(ks[20], (N_ODD, C_DH), 0.1),
        'subln_w': 1.0 + nrm(ks[21], (N_ODD, 2 * C_DH), 0.02),
        'w_out_odd': nrm(ks[22], (N_ODD, c_w, D_MODEL), c_w ** -0.5 * BETA),
        'w_up': nrm(ks[23], (DEPTH, D_MODEL, D_FF), D_MODEL ** -0.5),
        'w_down': nrm(ks[24], (DEPTH, D_FF, D_MODEL), D_FF ** -0.5 * BETA),
    }


def reference(x_prompt, x_sample, cache_k, cache_v, state_hgrn, page_table, ln_g, ln_b,
              w_in_even, w_s, b_s, vnorm_g, vnorm_b, lb_logits, gnorm_w, w_out_even,
              w_qkv, lam_q1, lam_k1, lam_q2, lam_k2, subln_w, w_out_odd, w_up, w_down):
    lb_all = jnp.cumsum(jax.nn.softmax(lb_logits.astype(jnp.float32), axis=0), axis=0)
    yp, ys = x_prompt, x_sample
    chunk_v_s, hg_p, hg_s, kp_l, vp_l, ks_l, vs_l = [], [], [], [], [], [], []
    for l in range(DEPTH):
        j = l // 2
        if l % 2 == 0:
            prm = (lb_all[l], w_in_even[j], w_s[j], b_s[j], vnorm_g[j], vnorm_b[j], gnorm_w[j], w_out_even[j])
            s0 = jnp.zeros((yp.shape[0], B_HEADS, B_DK, B_DV), jnp.float32)
            mp, _, sp = even_mixer(yp, s0, *prm)
            ms, v_rows, ss = even_mixer(ys, state_hgrn[j].astype(jnp.float32), *prm)
            hg_p.append(sp)
            hg_s.append(ss)
            chunk_v_s.append(v_rows)
        else:
            lam_init = lambda_init(l)
            lam = diff_lambda(lam_q1[j], lam_k1[j], lam_q2[j], lam_k2[j], lam_init)
            qp, kp, vp = qkv_project(yp, w_qkv[j])
            mp = diff_out(prompt_diff_attention(qp, kp, vp, lam), subln_w[j], w_out_odd[j], lam_init, yp.dtype)
            qs, kss, vss = qkv_project(ys, w_qkv[j])
            ms = diff_out(sample_diff_attention(qs, kss, vss, cache_k, cache_v, j, page_table, lam),
                          subln_w[j], w_out_odd[j], lam_init, ys.dtype)
            kp_l.append(kp)
            vp_l.append(vp)
            ks_l.append(kss)
            vs_l.append(vss)
        yp = layer_norm(ALPHA * yp + mp, ln_g[l, 0], ln_b[l, 0])
        yp = layer_norm(ALPHA * yp + sq_relu_mlp(yp, w_up[l], w_down[l]), ln_g[l, 1], ln_b[l, 1])
        ys = layer_norm(ALPHA * ys + ms, ln_g[l, 0], ln_b[l, 0])
        ys = layer_norm(ALPHA * ys + sq_relu_mlp(ys, w_up[l], w_down[l]), ln_g[l, 1], ln_b[l, 1])
    y_prompt, y_sample = yp, ys
    state_chunk_v_sample = jnp.stack(chunk_v_s, 0)
    state_hgrn_prompt = jnp.stack(hg_p, 0)
    state_hgrn_sample = jnp.stack(hg_s, 0)
    cache_k_prompt = jnp.stack(kp_l, 0)
    cache_v_prompt = jnp.stack(vp_l, 0)
    cache_k_sample = jnp.stack(ks_l, 0)
    cache_v_sample = jnp.stack(vs_l, 0)
    return (y_prompt, y_sample, state_chunk_v_sample, state_hgrn_prompt, state_hgrn_sample,
            cache_k_prompt, cache_v_prompt, cache_k_sample, cache_v_sample)
```

```python
import functools
import math

import jax
import jax.numpy as jnp
import numpy as np
from jax import lax
from jax.experimental import pallas as pl
from jax.experimental.pallas import tpu as pltpu

F32 = jnp.float32
BF16 = jnp.bfloat16

A_GROUPS = 4
A_CHUNK = 128
B_HEADS = 4
B_DK = 128
C_HEADS = 8
C_DH = 64
HEAD_W = 2 * C_DH
LN_EPS = 1e-5
RMS_EPS = 1e-5

LANES = 128
VMEM_LIMIT_BYTES = 56 * 1024 * 1024

GLA_CHUNK = 64
ATTN_BLOCK = 256
PAGES_PER_STEP = 4


def _params(*sem):
    return pltpu.CompilerParams(dimension_semantics=sem, vmem_limit_bytes=VMEM_LIMIT_BYTES)


def _full(shape):
    return pl.BlockSpec(shape, lambda *_: (0,) * len(shape), pipeline_mode=pl.Buffered(1))


def _pad_rows(a, rows):
    if a.shape[0] == rows:
        return a
    return jnp.concatenate([a, jnp.zeros((rows - a.shape[0],) + a.shape[1:], a.dtype)], axis=0)


def _sigmoid(x):
    return 1.0 / (1.0 + jnp.exp(-x))


def _gelu(x):
    return 0.5 * x * (1.0 + lax.erf(x * np.float32(math.sqrt(0.5))))


def _layer_norm(x, g, b):
    mu = jnp.mean(x, axis=-1, keepdims=True)
    xc = x - mu
    var = jnp.mean(xc * xc, axis=-1, keepdims=True)
    return xc * lax.rsqrt(var + LN_EPS) * g + b


def _row_to_col(row, n):
    eye = lax.broadcasted_iota(jnp.int32, (n, n), 0) == lax.broadcasted_iota(jnp.int32, (n, n), 1)
    return jnp.sum(jnp.where(eye, jnp.broadcast_to(row, (n, n)), 0.0), axis=1, keepdims=True)


def _even_in_kernel(x_ref, w_ref, lbl_ref, vg_ref, vb_ref,
                    u_ref, vn_ref, q_ref, k_ref, lf_ref, i_ref, g_ref, *, layer, aw, wq):
    z = jnp.dot(x_ref[...].astype(BF16), w_ref[...], preferred_element_type=F32)
    o0 = 2 * aw
    u_ref[...] = _gelu(z[:, :aw])
    vn_ref[...] = _layer_norm(_gelu(z[:, aw:o0]), vg_ref[...], vb_ref[...])
    q = z[:, o0:o0 + wq]
    q_ref[...] = q * _sigmoid(q)
    rows = [lbl_ref[r:r + 1, :] for r in range(lbl_ref.shape[0])]
    mx = functools.reduce(jnp.maximum, rows)
    es = [jnp.exp(r - mx) for r in rows]
    lb = functools.reduce(jnp.add, es[:layer + 1]) / functools.reduce(jnp.add, es)
    fg = lb + (1.0 - lb) * _sigmoid(z[:, o0 + wq:o0 + 2 * wq])
    k_ref[...] = 1.0 - fg
    lf_ref[...] = jnp.log(fg)
    i_ref[...] = z[:, o0 + 2 * wq:o0 + 3 * wq]
    g = z[:, o0 + 3 * wq:]
    g_ref[...] = g * _sigmoid(g)


def _even_in(x, w, lb_logits, vg, vb, layer, tm):
    r, d = x.shape
    n = w.shape[1]
    aw = vg.shape[1]
    wq = (n - 2 * aw) // 4
    row = lambda i: (i, 0)
    out = jax.ShapeDtypeStruct((r, aw), F32)
    return pl.pallas_call(
        functools.partial(_even_in_kernel, layer=layer, aw=aw, wq=wq),
        grid=(r // tm,),
        in_specs=[pl.BlockSpec((tm, d), row), _full(w.shape), _full(lb_logits.shape),
                  _full(vg.shape), _full(vb.shape)],
        out_specs=[pl.BlockSpec((tm, aw), row)] * 7,
        out_shape=[out] * 7,
        compiler_params=_params("parallel"),
        name="even_in",
    )(x, w, lb_logits, vg, vb)


def _gate_kernel(u_ref, vn_ref, w_ref, bias_ref, o_ref):
    c = u_ref.shape[0]
    gw = vn_ref.shape[1] // w_ref.shape[0]
    causal = lax.broadcasted_iota(jnp.int32, (c, c), 0) >= lax.broadcasted_iota(jnp.int32, (c, c), 1)
    vn = vn_ref[...].astype(BF16)
    parts = []
    for g in range(w_ref.shape[0]):
        w = jnp.where(causal, w_ref[g], 0.0).astype(BF16)
        parts.append(jnp.dot(w, vn[:, g * gw:(g + 1) * gw], preferred_element_type=F32))
    m = jnp.concatenate(parts, axis=1) + bias_ref[...]
    o_ref[...] = (u_ref[...] * m).astype(o_ref.dtype)


def _gate(u, vn, w, bias):
    r, aw = u.shape
    c = w.shape[1]
    row = lambda i: (i, 0)
    return pl.pallas_call(
        _gate_kernel,
        grid=(r // c,),
        in_specs=[pl.BlockSpec((c, aw), row), pl.BlockSpec((c, aw), row), _full(w.shape), _full(bias.shape)],
        out_specs=pl.BlockSpec((c, aw), row),
        out_shape=jax.ShapeDtypeStruct((r, aw), BF16),
        compiler_params=_params("parallel"),
        name="spatial_gate",
    )(u, vn, w, bias)


def _gla_chunk(q, k, lf, v, s, tri, sel, p_ref):
    c = q.shape[0]
    heads = len(s)
    rowi = lax.broadcasted_iota(jnp.int32, (c, B_DK), 0)
    bs, outs, new_s = [], [], []
    for h in range(heads):
        hs = slice(h * B_DK, (h + 1) * B_DK)
        b = jnp.dot(tri, lf[:, hs], preferred_element_type=F32, precision=lax.Precision.HIGHEST)
        bs.append(b)
        qh, kh = q[:, hs], k[:, hs]
        for j in range(c):
            d = jnp.exp(jnp.minimum(b - b[j:j + 1, :], 0.0))
            p = jnp.where(rowi >= j, qh * d * kh[j:j + 1, :], 0.0)
            p_ref[h * c:(h + 1) * c, j * B_DK:(j + 1) * B_DK] = p.astype(BF16)
    scores = jnp.dot(p_ref[...], sel, preferred_element_type=F32)
    for h in range(heads):
        hs = slice(h * B_DK, (h + 1) * B_DK)
        b = bs[h]
        vh = _pad_rows(v[:, hs], LANES).astype(BF16)
        o_inter = jnp.dot((q[:, hs] * jnp.exp(b)).astype(BF16), s[h].astype(BF16), preferred_element_type=F32)
        o_intra = jnp.dot(scores[h * c:(h + 1) * c, :].astype(BF16), vh, preferred_element_type=F32)
        outs.append(o_inter + o_intra)
        b_end = b[c - 1:c, :]
        k_hat = _pad_rows(k[:, hs] * jnp.exp(b_end - b), LANES)
        upd = jnp.dot(k_hat.T.astype(BF16), vh, preferred_element_type=F32)
        new_s.append(_row_to_col(jnp.exp(b_end), B_DK) * s[h] + upd)
    return jnp.concatenate(outs, axis=1), new_s


def _gla_out(o, g, gn_w):
    parts = []
    for h in range(o.shape[1] // B_DK):
        oh = o[:, h * B_DK:(h + 1) * B_DK]
        ms = jnp.mean(oh * oh, axis=-1, keepdims=True)
        parts.append(oh * lax.rsqrt(ms + RMS_EPS) * gn_w)
    return jnp.concatenate(parts, axis=1) * g


def _gla_prompt_kernel(q_ref, k_ref, lf_ref, v_ref, g_ref, gn_ref, tri_ref, sel_ref,
                       o_ref, sfin_ref, s_ref, p_ref):
    step = pl.program_id(0)

    @pl.when(step == 0)
    def _():
        s_ref[...] = jnp.zeros_like(s_ref)

    heads = s_ref.shape[0]
    o, new_s = _gla_chunk(q_ref[...], k_ref[...], lf_ref[...], v_ref[...],
                          [s_ref[h] for h in range(heads)], tri_ref[...], sel_ref[...], p_ref)
    for h in range(heads):
        s_ref[h] = new_s[h]
    o_ref[...] = _gla_out(o, g_ref[...], gn_ref[...]).astype(o_ref.dtype)

    @pl.when(step == pl.num_programs(0) - 1)
    def _():
        sfin_ref[...] = s_ref[...]


def _gla_consts(c):
    tri = np.tril(np.ones((c, c), np.float32))
    sel = np.zeros((c, B_DK, LANES), np.float32)
    sel[np.arange(c), :, np.arange(c)] = 1.0
    return jnp.asarray(tri), jnp.asarray(sel.reshape(c * B_DK, LANES), dtype=BF16)


def _gla_prompt(q, k, lf, v, g, gn_w):
    t, w = q.shape
    c = GLA_CHUNK
    heads = w // B_DK
    tri, sel = _gla_consts(c)
    row = lambda i: (i, 0)
    blk = pl.BlockSpec((c, w), row)
    return pl.pallas_call(
        _gla_prompt_kernel,
        grid=(t // c,),
        in_specs=[blk] * 5 + [_full(gn_w.shape), _full(tri.shape), _full(sel.shape)],
        out_specs=[blk, _full((heads, B_DK, B_DK))],
        out_shape=[jax.ShapeDtypeStruct((t, w), BF16), jax.ShapeDtypeStruct((heads, B_DK, B_DK), F32)],
        scratch_shapes=[pltpu.VMEM((heads, B_DK, B_DK), F32), pltpu.VMEM((heads * c, c * B_DK), BF16)],
        compiler_params=_params("arbitrary"),
        name="gla_prompt",
    )(q, k, lf, v, g, gn_w, tri, sel)


def _gla_sample_kernel(q_ref, k_ref, lf_ref, v_ref, g_ref, s0_ref, gn_ref, tri_ref, sel_ref,
                       o_ref, s1_ref, p_ref):
    heads = s0_ref.shape[1]
    n_new = q_ref.shape[0]
    c = tri_ref.shape[0]
    q, k, lf, v = (_pad_rows(r[...], c) for r in (q_ref, k_ref, lf_ref, v_ref))
    o, new_s = _gla_chunk(q, k, lf, v, [s0_ref[0, h] for h in range(heads)], tri_ref[...], sel_ref[...], p_ref)
    for h in range(heads):
        s1_ref[0, h] = new_s[h]
    o_ref[...] = _gla_out(o[:n_new], g_ref[...], gn_ref[...]).astype(o_ref.dtype)


def _gla_sample(q, k, lf, v, g, s0, gn_w, n_new):
    r, w = q.shape
    n, heads = s0.shape[:2]
    c = 16
    tri, sel = _gla_consts(c)
    row = lambda i: (i, 0)
    blk = pl.BlockSpec((n_new, w), row)
    sblk = pl.BlockSpec((1, heads, B_DK, B_DK), lambda i: (i, 0, 0, 0))
    return pl.pallas_call(
        _gla_sample_kernel,
        grid=(n,),
        in_specs=[blk] * 5 + [sblk, _full(gn_w.shape), _full(tri.shape), _full(sel.shape)],
        out_specs=[blk, sblk],
        out_shape=[jax.ShapeDtypeStruct((r, w), F32), jax.ShapeDtypeStruct(s0.shape, F32)],
        scratch_shapes=[pltpu.VMEM((heads * c, c * B_DK), BF16)],
        compiler_params=_params("parallel"),
        name="gla_sample",
    )(q, k, lf, v, g, s0, gn_w, tri, sel)


def _post_kernel(x_ref, ma_ref, mb_ref, woa_ref, wob_ref, wup_ref, wdn_ref, lng_ref, lnb_ref, o_ref, *, alpha):
    y = jnp.dot(ma_ref[...].astype(BF16), woa_ref[...], preferred_element_type=F32)
    y += jnp.dot(mb_ref[...].astype(BF16), wob_ref[...], preferred_element_type=F32)
    x1 = _layer_norm(alpha * x_ref[...] + y, lng_ref[0:1, :], lnb_ref[0:1, :])
    h = jnp.maximum(jnp.dot(x1.astype(BF16), wup_ref[...], preferred_element_type=F32), 0.0)
    y2 = jnp.dot((h * h).astype(BF16), wdn_ref[...], preferred_element_type=F32)
    o_ref[...] = _layer_norm(alpha * x1 + y2, lng_ref[1:2, :], lnb_ref[1:2, :])


def _post(x, parts, wo, wup, wdn, ln_g, ln_b, alpha, tm):
    r, d = x.shape
    half = wo.shape[0] // 2
    row = lambda i: (i, 0)
    woa, wob = wo[:half], wo[half:]
    (ma, ca), (mb, cb) = parts
    return pl.pallas_call(
        functools.partial(_post_kernel, alpha=alpha),
        grid=(r // tm,),
        in_specs=[pl.BlockSpec((tm, d), row), pl.BlockSpec((tm, half), lambda i: (i, ca)),
                  pl.BlockSpec((tm, half), lambda i: (i, cb)),
                  _full(woa.shape), _full(wob.shape), _full(wup.shape), _full(wdn.shape),
                  _full(ln_g.shape), _full(ln_b.shape)],
        out_specs=pl.BlockSpec((tm, d), row),
        out_shape=jax.ShapeDtypeStruct((r, d), F32),
        compiler_params=_params("parallel"),
        name="post_mlp",
    )(x, ma, mb, woa, wob, wup, wdn, ln_g, ln_b)


def _qkv_prompt_kernel(x_ref, w_ref, k_ref, v_ref, kb_ref, qt_ref, vt_ref, *, cw):
    z = jnp.dot(x_ref[...].astype(BF16), w_ref[...], preferred_element_type=F32)
    k = z[:, cw:2 * cw]
    v = z[:, 2 * cw:]
    k_ref[...] = k
    v_ref[...] = v
    kb_ref[...] = k.astype(BF16)
    qt_ref[0] = (z[:, :cw] * np.float32(C_DH ** -0.5)).T.astype(BF16)
    vt_ref[0] = v.T.astype(BF16)


def _qkv_prompt(x, w, tm):
    r, d = x.shape
    cw = w.shape[1] // 3
    nb = r // tm
    row = lambda i: (i, 0)
    slab = pl.BlockSpec((1, cw, tm), lambda i: (i, 0, 0))
    return pl.pallas_call(
        functools.partial(_qkv_prompt_kernel, cw=cw),
        grid=(nb,),
        in_specs=[pl.BlockSpec((tm, d), row), _full(w.shape)],
        out_specs=[pl.BlockSpec((tm, cw), row)] * 3 + [slab, slab],
        out_shape=[jax.ShapeDtypeStruct((r, cw), F32), jax.ShapeDtypeStruct((r, cw), F32),
                   jax.ShapeDtypeStruct((r, cw), BF16),
                   jax.ShapeDtypeStruct((nb, cw, tm), BF16), jax.ShapeDtypeStruct((nb, cw, tm), BF16)],
        compiler_params=_params("parallel"),
        name="qkv_prompt",
    )(x, w)


def _qkv_sample_kernel(x_ref, w_ref, q_ref, k_ref, v_ref, *, cw):
    z = jnp.dot(x_ref[...].astype(BF16), w_ref[...], preferred_element_type=F32)
    q_ref[...] = z[:, :cw] * np.float32(C_DH ** -0.5)
    k_ref[...] = z[:, cw:2 * cw]
    v_ref[...] = z[:, 2 * cw:]


def _qkv_sample(x, w, tm):
    r, d = x.shape
    cw = w.shape[1] // 3
    row = lambda i: (i, 0)
    out = jax.ShapeDtypeStruct((r, cw), F32)
    return pl.pallas_call(
        functools.partial(_qkv_sample_kernel, cw=cw),
        grid=(r // tm,),
        in_specs=[pl.BlockSpec((tm, d), row), _full(w.shape)],
        out_specs=[pl.BlockSpec((tm, cw), row)] * 3,
        out_shape=[out] * 3,
        compiler_params=_params("parallel"),
        name="qkv_sample",
    )(x, w)


def _diff_lambda(lq1_ref, lk1_ref, lq2_ref, lk2_ref, lam_init):
    a = jnp.sum(lq1_ref[...] * lk1_ref[...], axis=1, keepdims=True)
    b = jnp.sum(lq2_ref[...] * lk2_ref[...], axis=1, keepdims=True)
    return jnp.exp(a) - jnp.exp(b) + lam_init


def _attn_prompt_kernel(qt_ref, kb_ref, vt_ref, lq1_ref, lk1_ref, lq2_ref, lk2_ref, sw_ref,
                        o_ref, acc1_ref, acc2_ref, *, lam_init):
    i = pl.program_id(1)
    tq = qt_ref.shape[2]
    qt = qt_ref[0]
    first_half = lax.broadcasted_iota(jnp.int32, qt.shape, 0) < C_DH
    qa = jnp.where(first_half, qt, jnp.zeros_like(qt))
    qb = jnp.where(first_half, jnp.zeros_like(qt), qt)
    acc1_ref[...] = jnp.zeros_like(acc1_ref)
    acc2_ref[...] = jnp.zeros_like(acc2_ref)

    def block(j, carry, masked):
        kblk = kb_ref[pl.ds(pl.multiple_of(j * tq, tq), tq), :]
        vblk = vt_ref[j]
        new = []
        for (m, l), qh, acc_ref in ((carry[0:2], qa, acc1_ref), (carry[2:4], qb, acc2_ref)):
            s = jnp.dot(kblk, qh, preferred_element_type=F32)
            if masked:
                kpos = lax.broadcasted_iota(jnp.int32, s.shape, 0)
                qpos = lax.broadcasted_iota(jnp.int32, s.shape, 1)
                s = jnp.where(kpos <= qpos, s, -jnp.inf)
            m_new = jnp.maximum(m, jnp.max(s, axis=0, keepdims=True))
            a = jnp.exp(m - m_new)
            e = jnp.exp(s - m_new)
            l_new = a * l + jnp.sum(e, axis=0, keepdims=True)
            acc_ref[...] = a * acc_ref[...] + jnp.dot(vblk, e.astype(BF16), preferred_element_type=F32)
            new += [m_new, l_new]
        return tuple(new)

    neg = jnp.full((1, tq), -jnp.inf, F32)
    zero = jnp.zeros((1, tq), F32)
    carry = lax.fori_loop(0, i, lambda j, c: block(j, c, False), (neg, zero, neg, zero))
    _, l1, _, l2 = block(i, carry, True)
    lam = _diff_lambda(lq1_ref, lk1_ref, lq2_ref, lk2_ref, lam_init)
    o = acc1_ref[...] / l1 - lam * (acc2_ref[...] / l2)
    ms = jnp.mean(o * o, axis=0, keepdims=True)
    o = o * lax.rsqrt(ms + RMS_EPS) * sw_ref[...] * (1.0 - lam_init)
    o_ref[...] = o.T.astype(o_ref.dtype)


def _attn_prompt(qt, kb, vt, lam_params, sw_col, lam_init):
    nb, cw, tq = qt.shape
    t = kb.shape[0]
    heads = cw // HEAD_W
    return pl.pallas_call(
        functools.partial(_attn_prompt_kernel, lam_init=lam_init),
        grid=(heads, nb),
        in_specs=[pl.BlockSpec((1, HEAD_W, tq), lambda h, i: (i, h, 0)),
                  pl.BlockSpec((t, HEAD_W), lambda h, i: (0, h)),
                  pl.BlockSpec((nb, HEAD_W, tq), lambda h, i: (0, h, 0))]
                 + [_full(p.shape) for p in lam_params] + [_full(sw_col.shape)],
        out_specs=pl.BlockSpec((tq, HEAD_W), lambda h, i: (i, h)),
        out_shape=jax.ShapeDtypeStruct((t, cw), BF16),
        scratch_shapes=[pltpu.VMEM((HEAD_W, tq), F32), pltpu.VMEM((HEAD_W, tq), F32)],
        compiler_params=_params("parallel", "arbitrary"),
        name="attn_prompt",
    )(qt, kb, vt, *lam_params, sw_col)


def _attn_sample_kernel(pt_ref, *refs, lam_init, pages, n_new):
    k_refs, v_refs = refs[:pages], refs[pages:2 * pages]
    (qbd_ref, kn_ref, vn_ref, lq1_ref, lk1_ref, lq2_ref, lk2_ref, sw_ref,
     o_ref, m_ref, l_ref, acc_ref) = refs[2 * pages:]
    del pt_ref
    step = pl.program_id(1)
    ncol = qbd_ref.shape[3]
    cols_per_head = ncol // C_HEADS
    page = k_refs[0].shape[0] // C_HEADS

    @pl.when(step == 0)
    def _():
        m_ref[...] = jnp.full_like(m_ref, -jnp.inf)
        l_ref[...] = jnp.zeros_like(l_ref)
        acc_ref[...] = jnp.zeros_like(acc_ref)

    def update(s, v_of_head):
        m_new = jnp.maximum(m_ref[...], jnp.max(s, axis=0, keepdims=True))
        a = jnp.exp(m_ref[...] - m_new)
        e = jnp.exp(s - m_new)
        l_ref[...] = a * l_ref[...] + jnp.sum(e, axis=0, keepdims=True)
        m_ref[...] = m_new
        et = e.T.astype(BF16)
        pv = [jnp.dot(et[h * cols_per_head:(h + 1) * cols_per_head, :], v_of_head(h),
                      preferred_element_type=F32) for h in range(C_HEADS)]
        acc_ref[...] = _row_to_col(a, ncol) * acc_ref[...] + jnp.concatenate(pv, axis=0)

    def head_rows(ref, h):
        return ref[pl.ds(h, page, stride=C_HEADS), :].astype(BF16)

    s = jnp.zeros((pages * page, ncol), F32)
    for h in range(C_HEADS):
        kh = jnp.concatenate([head_rows(r, h) for r in k_refs], axis=0)
        s += jnp.dot(kh, qbd_ref[0, h], preferred_element_type=F32)
    update(s, lambda h: jnp.concatenate([head_rows(r, h) for r in v_refs], axis=0))

    @pl.when(step == pl.num_programs(1) - 1)
    def _():
        kn = _pad_rows(kn_ref[0], LANES).astype(BF16)
        vn = _pad_rows(vn_ref[0], LANES).astype(BF16)
        sn = jnp.zeros((kn.shape[0], ncol), F32)
        for h in range(C_HEADS):
            sn += jnp.dot(kn[:, h * HEAD_W:(h + 1) * HEAD_W], qbd_ref[0, h], preferred_element_type=F32)
        key = lax.broadcasted_iota(jnp.int32, sn.shape, 0)
        qry = jnp.bitwise_and(lax.broadcasted_iota(jnp.int32, sn.shape, 1), n_new - 1)
        sn = jnp.where(key <= qry, sn, -jnp.inf)
        update(sn, lambda h: vn[:, h * HEAD_W:(h + 1) * HEAD_W])
        lam = _diff_lambda(lq1_ref, lk1_ref, lq2_ref, lk2_ref, lam_init)
        p = acc_ref[...] / _row_to_col(l_ref[...], ncol)
        outs = []
        for h in range(C_HEADS):
            r0 = h * cols_per_head
            o = p[r0:r0 + n_new, :] - lam * p[r0 + n_new:r0 + 2 * n_new, :]
            ms = jnp.mean(o * o, axis=-1, keepdims=True)
            outs.append(o * lax.rsqrt(ms + RMS_EPS) * sw_ref[...] * (1.0 - lam_init))
        o_ref[...] = jnp.concatenate(outs, axis=1)


def _attn_sample(page_table, ck, cv, layer_slot, qbd, kn, vn, lam_params, sw_row, lam_init, n_new):
    n, n_pages = page_table.shape
    pages = PAGES_PER_STEP
    rows, hw = ck.shape[2:]
    ncol = qbd.shape[3]

    def page_spec(r):
        return pl.BlockSpec((None, None, rows, hw), lambda b, p, pt: (layer_slot, pt[b, p * pages + r], 0, 0))

    req3 = lambda shape: pl.BlockSpec((1,) + shape, lambda b, p, pt: (b, 0, 0))
    cst = lambda shape: pl.BlockSpec(shape, lambda b, p, pt: (0,) * len(shape))
    grid_spec = pltpu.PrefetchScalarGridSpec(
        num_scalar_prefetch=1,
        grid=(n, n_pages // pages),
        in_specs=[page_spec(r) for r in range(pages)] * 2
                 + [pl.BlockSpec((1,) + qbd.shape[1:], lambda b, p, pt: (b, 0, 0, 0)),
                    req3(kn.shape[1:]), req3(vn.shape[1:])]
                 + [cst(p.shape) for p in lam_params] + [cst(sw_row.shape)],
        out_specs=pl.BlockSpec((n_new, C_HEADS * HEAD_W), lambda b, p, pt: (b, 0)),
        scratch_shapes=[pltpu.VMEM((1, ncol), F32), pltpu.VMEM((1, ncol), F32), pltpu.VMEM((ncol, HEAD_W), F32)],
    )
    return pl.pallas_call(
        functools.partial(_attn_sample_kernel, lam_init=lam_init, pages=pages, n_new=n_new),
        grid_spec=grid_spec,
        out_shape=jax.ShapeDtypeStruct((n * n_new, C_HEADS * HEAD_W), F32),
        compiler_params=_params("parallel", "arbitrary"),
        name="attn_sample",
    )(page_table, *([ck] * pages), *([cv] * pages), qbd, kn, vn, *lam_params, sw_row)


def _block_diag_queries(q, n, n_new):
    q5 = q.reshape(n, n_new, C_HEADS, 2, C_DH).astype(BF16)
    eye_h = jnp.eye(C_HEADS, dtype=BF16)
    eye_2 = jnp.eye(2, dtype=BF16)
    out = jnp.einsum("bthjd,hg,ji->bhjdgit", q5, eye_h, eye_2)
    return out.reshape(n, C_HEADS, HEAD_W, C_HEADS * 2 * n_new)


def kernel(x_prompt, x_sample, cache_k, cache_v, state_hgrn, page_table, ln_g, ln_b, w_in_even, w_s, b_s,
           vnorm_g, vnorm_b, lb_logits, gnorm_w, w_out_even, w_qkv, lam_q1, lam_k1, lam_q2, lam_k2,
           subln_w, w_out_odd, w_up, w_down):
    depth = ln_g.shape[0]
    alpha = float((2 * depth) ** 0.25)
    nb, seq, d = x_prompt.shape
    assert nb == 1, "the prompt group is one sequence"
    n_dec, n_new, _ = x_sample.shape
    assert n_new == 8, "column layout of the sample attention packs 2 * n_new queries per head tile"
    assert seq % ATTN_BLOCK == 0 and seq % A_CHUNK == 0 and seq % GLA_CHUNK == 0
    yp = x_prompt.reshape(seq, d)
    ys = x_sample.reshape(n_dec * n_new, d)
    rs = ys.shape[0]
    tm_p = 512 if seq % 512 == 0 else ATTN_BLOCK
    aw = vnorm_g.shape[1]
    gw = aw // A_GROUPS
    row2 = lambda a: a.reshape(1, -1)

    chunk_v_s, hg_p, hg_s, kp_l, vp_l, ks_l, vs_l = [], [], [], [], [], [], []
    for l in range(depth):
        j = l // 2
        wup, wdn = w_up[l].astype(BF16), w_down[l].astype(BF16)
        if l % 2 == 0:
            w_in = w_in_even[j].astype(BF16)
            vg, vb, gn = row2(vnorm_g[j]), row2(vnorm_b[j]), row2(gnorm_w[j])
            wo = w_out_even[j].astype(BF16)
            u, vn, q, k, lf, iv, g = _even_in(yp, w_in, lb_logits, vg, vb, l, ATTN_BLOCK)
            bias_p = jnp.repeat(b_s[j].T, gw, axis=1)
            a_out = _gate(u, vn, w_s[j], bias_p)
            b_out, s_fin = _gla_prompt(q, k, lf, iv, g, gn)
            mp = ((a_out, 0), (b_out, 0))
            hg_p.append(s_fin[None])
            u, vn, q, k, lf, iv, g = _even_in(ys, w_in, lb_logits, vg, vb, l, rs)
            eye_n = jnp.eye(n_dec, dtype=F32)
            w_bd = jax.vmap(lambda w: jnp.kron(eye_n, w[:n_new, :n_new]))(w_s[j])
            bias_s = jnp.tile(bias_p[:n_new], (n_dec, 1))
            a_out = _gate(u, vn, w_bd, bias_s)
            b_out, s_new = _gla_sample(q, k, lf, iv, g, state_hgrn[j].astype(F32), gn, n_new)
            ms = ((a_out, 0), (b_out, 0))
            hg_s.append(s_new)
            chunk_v_s.append(vn.reshape(n_dec, n_new, aw))
        else:
            lam_init = 0.8 - 0.6 * math.exp(-0.3 * l)
            wqkv = w_qkv[j].astype(BF16)
            wo = w_out_odd[j].astype(BF16)
            cw = wo.shape[0]
            lam_params = [row2(p[j]).astype(F32) for p in (lam_q1, lam_k1, lam_q2, lam_k2)]
            sw = subln_w[j].astype(F32)
            kf, vf, kb, qt, vt = _qkv_prompt(yp, wqkv, ATTN_BLOCK)
            o = _attn_prompt(qt, kb, vt, lam_params, sw.reshape(-1, 1), lam_init)
            mp = ((o, 0), (o, 1))
            kp_l.append(kf.reshape(nb, seq, C_HEADS, HEAD_W))
            vp_l.append(vf.reshape(nb, seq, C_HEADS, HEAD_W))
            qs, ksn, vsn = _qkv_sample(ys, wqkv, rs)
            qbd = _block_diag_queries(qs, n_dec, n_new)
            kn = ksn.reshape(n_dec, n_new, cw)
            vn_ = vsn.reshape(n_dec, n_new, cw)
            n_odd, n_phys, page = cache_k.shape[:3]
            ck = cache_k.reshape(n_odd, n_phys, page * C_HEADS, HEAD_W)
            cv = cache_v.reshape(n_odd, n_phys, page * C_HEADS, HEAD_W)
            o = _attn_sample(page_table, ck, cv, j, qbd, kn, vn_, lam_params, sw.reshape(1, -1), lam_init, n_new)
            ms = ((o, 0), (o, 1))
            ks_l.append(ksn.reshape(n_dec, n_new, C_HEADS, HEAD_W))
            vs_l.append(vsn.reshape(n_dec, n_new, C_HEADS, HEAD_W))
        yp = _post(yp, mp, wo, wup, wdn, ln_g[l], ln_b[l], alpha, tm_p)
        ys = _post(ys, ms, wo, wup, wdn, ln_g[l], ln_b[l], alpha, rs)
    return (yp.reshape(nb, seq, d), ys.reshape(n_dec, n_new, d),
            jnp.stack(chunk_v_s, 0), jnp.stack(hg_p, 0), jnp.stack(hg_s, 0),
            jnp.stack(kp_l, 0), jnp.stack(vp_l, 0), jnp.stack(ks_l, 0), jnp.stack(vs_l, 0))
```
